```python
import math
import jax, jax.numpy as jnp
from jax import lax
import numpy as np

D_MODEL = 1024
BATCH = 8
SEQ = 2048
DEPTH = 1

N_HEADS = 8
HEAD_DIM = 64
V_DIM = 2 * HEAD_DIM
QK_WIDTH = N_HEADS * 2 * HEAD_DIM
ATTN_WIDTH = N_HEADS * V_DIM
Q_BLOCK = 128
SSM_WIDTH = D_MODEL // 2
SSM_GROUP = 16
SSM_GROUPS = SSM_WIDTH // SSM_GROUP
SSM_STATE = 64
DT_MIN = 1e-3
DT_MAX = 1e-1
D_FF = 4 * D_MODEL
N_BUCKETS = 32
MAX_DISTANCE = 128
EPS = 1e-6
IN_WIDTH = 2 * QK_WIDTH + ATTN_WIDTH + SSM_WIDTH + 2 * D_MODEL
SPLITS = [QK_WIDTH, 2 * QK_WIDTH, 2 * QK_WIDTH + ATTN_WIDTH,
          2 * QK_WIDTH + ATTN_WIDTH + SSM_WIDTH,
          2 * QK_WIDTH + ATTN_WIDTH + SSM_WIDTH + D_MODEL]

kernel_name = 'hybrid_diffattn_s5_gated_block'


def rmsnorm(x, g):
    xf = x.astype(jnp.float32)
    y = xf * lax.rsqrt(jnp.mean(xf * xf, axis=-1, keepdims=True) + EPS)
    return (y * g.astype(jnp.float32)).astype(x.dtype)


def lambda_init_fn(layer):
    return 0.8 - 0.6 * math.exp(-0.3 * layer)


def t5_bucket(rel):
    n = jnp.maximum(rel, 0)
    max_exact = N_BUCKETS // 2
    is_small = n < max_exact
    large = max_exact + (jnp.log(jnp.maximum(n, 1).astype(jnp.float32) / max_exact)
                         / math.log(MAX_DISTANCE / max_exact)
                         * (N_BUCKETS - max_exact)).astype(jnp.int32)
    large = jnp.minimum(large, N_BUCKETS - 1)
    return jnp.where(is_small, n, large)


def diff_attention(q, k, v, rel_bias, lam, subln_g, lam_init):
    b, l = q.shape[0], q.shape[1]
    nb = l // Q_BLOCK
    q = q.reshape(b, l, N_HEADS, 2, HEAD_DIM).transpose(0, 2, 3, 1, 4)
    k = k.reshape(b, l, N_HEADS, 2, HEAD_DIM).transpose(0, 2, 3, 1, 4)
    v = v.reshape(b, l, N_HEADS, V_DIM).transpose(0, 2, 1, 3)
    q_blocks = q.reshape(b, N_HEADS, 2, nb, Q_BLOCK, HEAD_DIM).transpose(3, 0, 1, 2, 4, 5)
    k_pos = jnp.arange(l)
    scale = HEAD_DIM ** -0.5

    def one_block(args):
        q_blk, i = args
        q_pos = i * Q_BLOCK + jnp.arange(Q_BLOCK)
        rel = q_pos[:, None] - k_pos[None, :]
        bias = rel_bias[t5_bucket(rel)].astype(jnp.float32).transpose(2, 0, 1)
        s = jnp.einsum('bhmqd,bhmkd->bhmqk', q_blk, k).astype(jnp.float32) * scale
        s = s + bias[None, :, None]
        s = jnp.where((rel >= 0)[None, None, None], s, -jnp.inf)
        p = jax.nn.softmax(s, axis=-1)
        w = p[:, :, 0] - lam * p[:, :, 1]
        return jnp.einsum('bhqk,bhkv->bhqv', w.astype(v.dtype), v)

    o = lax.map(one_block, (q_blocks, jnp.arange(nb)))
    o = o.transpose(1, 0, 3, 2, 4).reshape(b, l, N_HEADS, V_DIM)
    o = rmsnorm(o, subln_g) * (1.0 - lam_init)
    return o.reshape(b, l, ATTN_WIDTH)


def _complex_affine_combine(e1, e2):
    a1r, a1i, b1r, b1i = e1
    a2r, a2i, b2r, b2i = e2
    ar = a2r * a1r - a2i * a1i
    ai = a2r * a1i + a2i * a1r
    br = a2r * b1r - a2i * b1i + b2r
    bi = a2r * b1i + a2i * b1r + b2i
    return (ar, ai, br, bi)


def s5_ssm(u, a_re, a_im, log_dt, b_re, b_im, c_re, c_im, d_skip):
    bsz, l = u.shape[0], u.shape[1]
    uf = u.astype(jnp.float32).reshape(bsz, l, SSM_GROUPS, SSM_GROUP)
    dt = jnp.exp(log_dt.astype(jnp.float32))[:, None]
    ar = a_re.astype(jnp.float32)
    ai = a_im.astype(jnp.float32)
    mag = jnp.exp(ar * dt)
    lb_re = mag * jnp.cos(ai * dt)
    lb_im = mag * jnp.sin(ai * dt)
    nr = lb_re - 1.0
    ni = lb_im
    den = ar * ar + ai * ai
    cr = ((nr * ar + ni * ai) / den)[..., None]
    ci = ((ni * ar - nr * ai) / den)[..., None]
    br = b_re.astype(jnp.float32)
    bi = b_im.astype(jnp.float32)
    bb_re = cr * br - ci * bi
    bb_im = cr * bi + ci * br
    bu_re = jnp.einsum('blgh,gph->blgp', uf, bb_re)
    bu_im = jnp.einsum('blgh,gph->blgp', uf, bb_im)
    lam_re = jnp.broadcast_to(lb_re, bu_re.shape)
    lam_im = jnp.broadcast_to(lb_im, bu_im.shape)
    _, _, s_re, s_im = lax.associative_scan(
        _complex_affine_combine, (lam_re, lam_im, bu_re, bu_im), axis=1)
    y = (jnp.einsum('blgp,ghp->blgh', s_re, c_re.astype(jnp.float32))
         - jnp.einsum('blgp,ghp->blgh', s_im, c_im.astype(jnp.float32))
         + d_skip.astype(jnp.float32).reshape(SSM_GROUPS, SSM_GROUP) * uf)
    return y.reshape(bsz, l, SSM_WIDTH).astype(u.dtype)


def setup_inputs(seed: int = 0) -> dict:
    key = jax.random.key(seed)
    ks = jax.random.split(key, 24)
    f32 = jnp.float32
    nrm = lambda k, shape, s: jax.random.normal(k, shape, f32) * s
    n_idx = jnp.arange(SSM_STATE, dtype=f32)
    a_re = -0.5 + nrm(ks[9], (DEPTH, SSM_GROUPS, SSM_STATE), 0.01)
    a_im = math.pi * n_idx[None, None, :] + nrm(ks[10], (DEPTH, SSM_GROUPS, SSM_STATE), 0.01)
    log_dt = jax.random.uniform(ks[11], (DEPTH, SSM_GROUPS), f32,
                                math.log(DT_MIN), math.log(DT_MAX))
    b_scale = (0.5 / SSM_GROUP) ** 0.5
    c_scale = (0.5 / SSM_STATE) ** 0.5
    return {
        'x': jax.random.normal(ks[0], (BATCH, SEQ, D_MODEL), f32),
        'g_mix': 1.0 + nrm(ks[1], (DEPTH, D_MODEL), 0.02),
        'w_in': nrm(ks[2], (DEPTH, D_MODEL, IN_WIDTH), D_MODEL ** -0.5),
        'lambda_q1': nrm(ks[3], (DEPTH, HEAD_DIM), 0.1),
        'lambda_k1': nrm(ks[4], (DEPTH, HEAD_DIM), 0.1),
        'lambda_q2': nrm(ks[5], (DEPTH, HEAD_DIM), 0.1),
        'lambda_k2': nrm(ks[6], (DEPTH, HEAD_DIM), 0.1),
        'subln_g': 1.0 + nrm(ks[7], (DEPTH, V_DIM), 0.02),
        'rel_bias': nrm(ks[8], (N_BUCKETS, N_HEADS), 0.5),
        'ssm_a_re': a_re,
        'ssm_a_im': a_im,
        'ssm_log_dt': log_dt,
        'ssm_b_re': nrm(ks[12], (DEPTH, SSM_GROUPS, SSM_STATE, SSM_GROUP), b_scale),
        'ssm_b_im': nrm(ks[13], (DEPTH, SSM_GROUPS, SSM_STATE, SSM_GROUP), b_scale),
        'ssm_c_re': nrm(ks[14], (DEPTH, SSM_GROUPS, SSM_GROUP, SSM_STATE), c_scale),
        'ssm_c_im': nrm(ks[15], (DEPTH, SSM_GROUPS, SSM_GROUP, SSM_STATE), c_scale),
        'ssm_d': nrm(ks[16], (DEPTH, SSM_WIDTH), 1.0),
        'w_glu': nrm(ks[17], (DEPTH, SSM_WIDTH, 2 * D_MODEL), SSM_WIDTH ** -0.5),
        'w_out': nrm(ks[18], (DEPTH, D_MODEL, D_MODEL), D_MODEL ** -0.5),
        'g_mlp': 1.0 + nrm(ks[19], (DEPTH, D_MODEL), 0.02),
        'w1': nrm(ks[20], (DEPTH, D_MODEL, D_FF), D_MODEL ** -0.5),
        'w2': nrm(ks[21], (DEPTH, D_FF, D_MODEL), D_FF ** -0.5),
        'g_final': 1.0 + nrm(ks[22], (D_MODEL,), 0.02),
    }


def reference(x, g_mix, w_in, lambda_q1, lambda_k1, lambda_q2, lambda_k2, subln_g,
              rel_bias, ssm_a_re, ssm_a_im, ssm_log_dt, ssm_b_re, ssm_b_im,
              ssm_c_re, ssm_c_im, ssm_d, w_glu, w_out, g_mlp, w1, w2, g_final):
    for layer in range(DEPTH):
        h = rmsnorm(x, g_mix[layer])
        proj = h @ w_in[layer]
        q, k, v, u, gate_a, gate_s = jnp.split(proj, SPLITS, axis=-1)
        lam_init = lambda_init_fn(layer)
        lam = (jnp.exp(jnp.sum(lambda_q1[layer].astype(jnp.float32) * lambda_k1[layer].astype(jnp.float32)))
               - jnp.exp(jnp.sum(lambda_q2[layer].astype(jnp.float32) * lambda_k2[layer].astype(jnp.float32)))
               + lam_init)
        attn = diff_attention(q, k, v, rel_bias, lam, subln_g[layer], lam_init)
        y = s5_ssm(u, ssm_a_re[layer], ssm_a_im[layer], ssm_log_dt[layer],
                   ssm_b_re[layer], ssm_b_im[layer], ssm_c_re[layer], ssm_c_im[layer],
                   ssm_d[layer])
        z = jax.nn.gelu(y) @ w_glu[layer]
        z_val, z_gate = jnp.split(z, 2, axis=-1)
        ssm = z_val * jax.nn.sigmoid(z_gate)
        mixed = jax.nn.sigmoid(gate_a) * attn + jax.nn.sigmoid(gate_s) * ssm
        x = x + mixed @ w_out[layer]
        h = rmsnorm(x, g_mlp[layer])
        x = x + jnp.square(jax.nn.relu(h @ w1[layer])) @ w2[layer]
    return rmsnorm(x, g_final)
```

```python
import functools
import math

import jax
import jax.numpy as jnp
import numpy as np
from jax import lax
from jax.experimental import pallas as pl
from jax.experimental.pallas import tpu as pltpu

D_MODEL = 1024
N_HEADS = 8
HEAD_DIM = 64
V_DIM = 2 * HEAD_DIM
QK_WIDTH = N_HEADS * 2 * HEAD_DIM
ATTN_WIDTH = N_HEADS * V_DIM
SSM_WIDTH = D_MODEL // 2
SSM_GROUP = 16
SSM_GROUPS = SSM_WIDTH // SSM_GROUP
SSM_STATE = 64
N_STATES = SSM_GROUPS * SSM_STATE
D_FF = 4 * D_MODEL
N_BUCKETS = 32
MAX_DISTANCE = 128
EPS = 1e-6

BF16 = jnp.bfloat16
F32 = jnp.float32

SUBLANES = 8
LANES = 128
VMEM_LIMIT_BYTES = 56 * 1024 * 1024

PROJ_TM = 1024
PROJ_TN = 512
ATT_T = 256
SSM_TL = 64
SSM_BLK = 4
MLP_TM = 512
MLP_TF = 512


def _t5_bucket_table(n_rel):
    n = np.arange(n_rel)
    max_exact = N_BUCKETS // 2
    nf = np.maximum(n, 1).astype(np.float32)
    large = max_exact + (np.log(nf / np.float32(max_exact)) / np.float32(math.log(MAX_DISTANCE / max_exact))
                         * np.float32(N_BUCKETS - max_exact)).astype(np.int32)
    large = np.minimum(large, N_BUCKETS - 1)
    return np.where(n < max_exact, n, large).astype(np.int32)


def _inproj_kernel(x_ref, g_ref, w_ref, proj_ref, u_ref, h_ref, *, n_col_steps):
    j = pl.program_id(1)

    @pl.when(j == 0)
    def _():
        x = x_ref[...]
        ms = jnp.mean(x * x, axis=-1, keepdims=True)
        h_ref[...] = (x * lax.rsqrt(ms + EPS) * g_ref[...]).astype(BF16)

    acc = jnp.dot(h_ref[...], w_ref[...], preferred_element_type=F32)

    @pl.when(j < n_col_steps - 1)
    def _():
        proj_ref[...] = acc.astype(BF16)

    @pl.when(j == n_col_steps - 1)
    def _():
        u_ref[...] = acc


def _inproj(x2d, g_mix, w_cat, batch, seq):
    t, d = x2d.shape
    n_main = (w_cat.shape[1] - SSM_WIDTH) // PROJ_TN
    n_col_steps = n_main + 1
    tiles_per_seq = seq // PROJ_TM
    return pl.pallas_call(
        functools.partial(_inproj_kernel, n_col_steps=n_col_steps),
        grid=(t // PROJ_TM, n_col_steps),
        in_specs=[
            pl.BlockSpec((PROJ_TM, d), lambda i, j: (i, 0)),
            pl.BlockSpec((1, d), lambda i, j: (0, 0)),
            pl.BlockSpec((d, PROJ_TN), lambda i, j: (0, j)),
        ],
        out_specs=[
            pl.BlockSpec((PROJ_TM, PROJ_TN), lambda i, j: (i, jnp.minimum(j, n_main - 1))),
            pl.BlockSpec((PROJ_TM, SSM_WIDTH), lambda i, j: (i % tiles_per_seq, i // tiles_per_seq)),
        ],
        out_shape=[
            jax.ShapeDtypeStruct((t, n_main * PROJ_TN), BF16),
            jax.ShapeDtypeStruct((seq, batch * SSM_WIDTH), F32),
        ],
        scratch_shapes=[pltpu.VMEM((PROJ_TM, d), BF16)],
        compiler_params=pltpu.CompilerParams(
            dimension_semantics=("arbitrary", "arbitrary"), vmem_limit_bytes=VMEM_LIMIT_BYTES),
        name="inproj",
    )(x2d, g_mix.reshape(1, d), w_cat)


def _bias_kernel(rb_ref, idx_ref, o_ref):
    h = pl.program_id(0)
    idx = idx_ref[...]
    out = jnp.zeros(idx.shape, F32)
    for b in range(N_BUCKETS):
        out = jnp.where(idx == b, rb_ref[b, h], out)
    o_ref[...] = jnp.where(idx < 0, -jnp.inf, out)


def _bias_tiles(rel_bias, bucket_idx):
    t = bucket_idx.shape[-1]
    return pl.pallas_call(
        _bias_kernel,
        grid=(N_HEADS,),
        in_specs=[
            pl.BlockSpec(memory_space=pltpu.SMEM),
            pl.BlockSpec((2, t, t), lambda h: (0, 0, 0)),
        ],
        out_specs=pl.BlockSpec((None, 2, t, t), lambda h: (h, 0, 0, 0)),
        out_shape=jax.ShapeDtypeStruct((N_HEADS, 2, t, t), F32),
        name="bias_tiles",
    )(rel_bias, bucket_idx)


def _attn_kernel(lam_ref, rb_ref, q_ref, k_ref, v_ref, bias_ref, g_ref, o_ref,
                 acc_ref, m_ref, l_ref, *, far_bucket, out_scale):
    h = pl.program_id(1)
    qi = pl.program_id(2)
    t = ATT_T

    q = q_ref[...] * (HEAD_DIM ** -0.5)
    lane = lax.broadcasted_iota(jnp.int32, q.shape, 1)
    zero = jnp.zeros_like(q)
    qq = jnp.concatenate([jnp.where(lane < HEAD_DIM, q, zero), jnp.where(lane >= HEAD_DIM, q, zero)], axis=0)

    def scores(j):
        start = pl.multiple_of(j * t, t)
        kt = k_ref[pl.ds(start, t), :]
        return lax.dot_general(qq, kt, (((1,), (1,)), ((), ())), preferred_element_type=F32), start

    s, start = scores(qi)
    s = (s.reshape(2, t, t) + bias_ref[0][None]).reshape(2 * t, t)
    m = jnp.max(s, axis=1, keepdims=True)
    p = jnp.exp(s - m)
    m_ref[...] = m
    l_ref[...] = jnp.sum(p, axis=1, keepdims=True)
    acc_ref[...] = jnp.dot(p.astype(BF16), v_ref[pl.ds(start, t), :], preferred_element_type=F32)

    def online_step(s, start, shift):
        m_prev = m_ref[...]
        m_new = jnp.maximum(m_prev, jnp.max(s, axis=1, keepdims=True) + shift)
        alpha = jnp.exp(m_prev - m_new)
        p = jnp.exp(s - (m_new - shift))
        l_ref[...] = alpha * l_ref[...] + jnp.sum(p, axis=1, keepdims=True)
        acc_ref[...] = alpha * acc_ref[...] + jnp.dot(
            p.astype(BF16), v_ref[pl.ds(start, t), :], preferred_element_type=F32)
        m_ref[...] = m_new

    @pl.when(qi >= 1)
    def _():
        s, start = scores(qi - 1)
        s = (s.reshape(2, t, t) + bias_ref[1][None]).reshape(2 * t, t)
        online_step(s, start, 0.0)

    far = rb_ref[far_bucket, h]

    def far_body(j, carry):
        s, start = scores(j)
        online_step(s, start, far)
        return carry

    lax.fori_loop(0, jnp.maximum(qi - 1, 0), far_body, 0)

    o = acc_ref[...] / l_ref[...]
    od = o[:t] - lam_ref[0] * o[t:]
    ms = jnp.mean(od * od, axis=-1, keepdims=True)
    o_ref[...] = (od * lax.rsqrt(ms + EPS) * g_ref[...] * out_scale).astype(o_ref.dtype)


def _attention(proj, lam, rel_bias, bias_tiles, subln_g, batch, seq, far_bucket, out_scale):
    t = ATT_T
    nq = seq // t
    rows = batch * seq
    kernel = functools.partial(_attn_kernel, far_bucket=far_bucket, out_scale=out_scale)
    return pl.pallas_call(
        kernel,
        grid=(batch, N_HEADS, nq),
        in_specs=[
            pl.BlockSpec(memory_space=pltpu.SMEM),
            pl.BlockSpec(memory_space=pltpu.SMEM),
            pl.BlockSpec((t, V_DIM), lambda b, h, i: (b * nq + i, h)),
            pl.BlockSpec((seq, V_DIM), lambda b, h, i: (b, N_HEADS + h)),
            pl.BlockSpec((seq, V_DIM), lambda b, h, i: (b, 2 * N_HEADS + h)),
            pl.BlockSpec((None, 2, t, t), lambda b, h, i: (h, 0, 0, 0)),
            pl.BlockSpec((1, V_DIM), lambda b, h, i: (0, 0)),
        ],
        out_specs=pl.BlockSpec((t, V_DIM), lambda b, h, i: (b * nq + i, h)),
        out_shape=jax.ShapeDtypeStruct((rows, ATTN_WIDTH), BF16),
        scratch_shapes=[
            pltpu.VMEM((2 * t, V_DIM), F32),
            pltpu.VMEM((2 * t, 1), F32),
            pltpu.VMEM((2 * t, 1), F32),
        ],
        compiler_params=pltpu.CompilerParams(
            dimension_semantics=("arbitrary", "arbitrary", "arbitrary"), vmem_limit_bytes=VMEM_LIMIT_BYTES),
        name="diff_attn",
    )(lam, rel_bias, proj, proj, proj, bias_tiles, subln_g.reshape(1, V_DIM))


def _ssm_kernel(u_ref, wb_ref, lam_ref, wcr_ref, wci_ref, d_ref, wglu_ref, o_ref,
                sre_ref, sim_ref, h_ref, y_ref, res_ref, *, batch):
    tl = SSM_TL
    blk_in = SSM_WIDTH // SSM_BLK
    blk_st = N_STATES // SSM_BLK

    @pl.when(pl.program_id(0) == 0)
    def _():
        h_ref[...] = jnp.zeros_like(h_ref)

    u = u_ref[...]
    ub = u.astype(BF16)
    for blk in range(SSM_BLK):
        r = jnp.dot(ub[:, blk * blk_in:(blk + 1) * blk_in], wb_ref[blk], preferred_element_type=F32)
        sre_ref[:, blk * blk_st:(blk + 1) * blk_st] = r[:, :blk_st]
        sim_ref[:, blk * blk_st:(blk + 1) * blk_st] = r[:, blk_st:]

    for blk in range(SSM_BLK):
        cs = slice(blk * blk_st, (blk + 1) * blk_st)
        lr = lam_ref[0, :, cs]
        li = lam_ref[1, :, cs]

        def body(tt, carry, cs=cs, lr=lr, li=li):
            hr, hi = carry
            row = pl.multiple_of(tt * batch, batch)
            nr = lr * hr - li * hi + sre_ref[pl.ds(row, batch), cs]
            ni = lr * hi + li * hr + sim_ref[pl.ds(row, batch), cs]
            sre_ref[pl.ds(row, batch), cs] = nr
            sim_ref[pl.ds(row, batch), cs] = ni
            return nr, ni

        hr, hi = lax.fori_loop(0, tl, body, (h_ref[0, :, cs], h_ref[1, :, cs]), unroll=8)
        h_ref[0, :, cs] = hr
        h_ref[1, :, cs] = hi

    for blk in range(SSM_BLK):
        cs = slice(blk * blk_st, (blk + 1) * blk_st)
        yb = (jnp.dot(sre_ref[:, cs].astype(BF16), wcr_ref[blk], preferred_element_type=F32)
              + jnp.dot(sim_ref[:, cs].astype(BF16), wci_ref[blk], preferred_element_type=F32))
        y_ref[:, blk * blk_in:(blk + 1) * blk_in] = yb
    y = y_ref[...] + d_ref[...] * u
    z = jnp.dot(jax.nn.gelu(y).astype(BF16), wglu_ref[...], preferred_element_type=F32)
    res = z[:, :D_MODEL] * jax.nn.sigmoid(z[:, D_MODEL:])
    for c in range(D_MODEL // LANES):
        res_ref[c] = res[:, c * LANES:(c + 1) * LANES]
    for b in range(batch):
        for c in range(D_MODEL // LANES):
            o_ref[b, :, c * LANES:(c + 1) * LANES] = res_ref[c, pl.ds(b, tl, stride=batch), :].astype(o_ref.dtype)


def _ssm(u_tb, wb, lam_b, wcr, wci, d_skip, w_glu, batch, seq):
    rows = SSM_TL * batch
    blk_in = SSM_WIDTH // SSM_BLK
    blk_st = N_STATES // SSM_BLK
    const3 = lambda i: (0, 0, 0)
    const2 = lambda i: (0, 0)
    return pl.pallas_call(
        functools.partial(_ssm_kernel, batch=batch),
        grid=(seq // SSM_TL,),
        in_specs=[
            pl.BlockSpec((rows, SSM_WIDTH), lambda i: (i, 0)),
            pl.BlockSpec((SSM_BLK, blk_in, 2 * blk_st), const3),
            pl.BlockSpec((2, batch, N_STATES), const3),
            pl.BlockSpec((SSM_BLK, blk_st, blk_in), const3),
            pl.BlockSpec((SSM_BLK, blk_st, blk_in), const3),
            pl.BlockSpec((1, SSM_WIDTH), const2),
            pl.BlockSpec((SSM_WIDTH, 2 * D_MODEL), const2),
        ],
        out_specs=pl.BlockSpec((batch, SSM_TL, D_MODEL), lambda i: (0, i, 0)),
        out_shape=jax.ShapeDtypeStruct((batch, seq, D_MODEL), BF16),
        scratch_shapes=[
            pltpu.VMEM((rows, N_STATES), F32),
            pltpu.VMEM((rows, N_STATES), F32),
            pltpu.VMEM((2, batch, N_STATES), F32),
            pltpu.VMEM((rows, SSM_WIDTH), F32),
            pltpu.VMEM((D_MODEL // LANES, rows, LANES), F32),
        ],
        compiler_params=pltpu.CompilerParams(
            dimension_semantics=("arbitrary",), vmem_limit_bytes=VMEM_LIMIT_BYTES),
        name="s5_ssm",
    )(u_tb, wb, lam_b, wcr, wci, d_skip.reshape(1, SSM_WIDTH), w_glu)


def _ssm_params(a_re, a_im, log_dt, b_re, b_im, c_re, c_im, batch):
    dt = jnp.exp(log_dt.astype(F32))[:, None]
    ar = a_re.astype(F32)
    ai = a_im.astype(F32)
    mag = jnp.exp(ar * dt)
    lb_re = mag * jnp.cos(ai * dt)
    lb_im = mag * jnp.sin(ai * dt)
    nr = lb_re - 1.0
    ni = lb_im
    den = ar * ar + ai * ai
    cr = ((nr * ar + ni * ai) / den)[..., None]
    ci = ((ni * ar - nr * ai) / den)[..., None]
    br = b_re.astype(F32)
    bi = b_im.astype(F32)
    bb_re = cr * br - ci * bi
    bb_im = cr * bi + ci * br

    gpb = SSM_GROUPS // SSM_BLK
    eye = jnp.eye(gpb, dtype=F32)

    def pack_b(bb):
        x = bb.reshape(SSM_BLK, gpb, SSM_STATE, SSM_GROUP)
        x = jnp.einsum('kgph,gf->kghfp', x, eye)
        return x.reshape(SSM_BLK, gpb * SSM_GROUP, gpb * SSM_STATE)

    def pack_c(c):
        x = c.astype(F32).reshape(SSM_BLK, gpb, SSM_GROUP, SSM_STATE)
        x = jnp.einsum('kghp,gf->kgpfh', x, eye)
        return x.reshape(SSM_BLK, gpb * SSM_STATE, gpb * SSM_GROUP)

    wb = jnp.concatenate([pack_b(bb_re), pack_b(bb_im)], axis=-1).astype(BF16)
    wcr = pack_c(c_re).astype(BF16)
    wci = (-pack_c(c_im)).astype(BF16)
    lam_b = jnp.stack([jnp.broadcast_to(lb_re.reshape(1, N_STATES), (batch, N_STATES)),
                       jnp.broadcast_to(lb_im.reshape(1, N_STATES), (batch, N_STATES))])
    return wb, lam_b, wcr, wci


def _mix_mlp_kernel(x_ref, attn_ref, ssm_ref, ga_ref, gs_ref, wout_ref, gmlp_ref, w1_ref, w2_ref,
                    gfin_ref, o_ref, x1_ref, h_ref, acc_ref, *, n_ff_steps, final_norm):
    j = pl.program_id(1)

    @pl.when(j == 0)
    def _():
        mixed = (jax.nn.sigmoid(ga_ref[...].astype(F32)) * attn_ref[...].astype(F32)
                 + jax.nn.sigmoid(gs_ref[...].astype(F32)) * ssm_ref[...].astype(F32))
        x1 = x_ref[...] + jnp.dot(mixed.astype(BF16), wout_ref[...], preferred_element_type=F32)
        x1_ref[...] = x1
        ms = jnp.mean(x1 * x1, axis=-1, keepdims=True)
        h_ref[...] = (x1 * lax.rsqrt(ms + EPS) * gmlp_ref[...]).astype(BF16)

    a = jnp.dot(h_ref[...], w1_ref[...], preferred_element_type=F32)
    a = jnp.square(jnp.maximum(a, 0.0)).astype(BF16)
    part = jnp.dot(a, w2_ref[...], preferred_element_type=F32)

    @pl.when(j == 0)
    def _():
        acc_ref[...] = part

    @pl.when(j > 0)
    def _():
        acc_ref[...] += part

    @pl.when(j == n_ff_steps - 1)
    def _():
        x2 = x1_ref[...] + acc_ref[...]
        if final_norm:
            ms = jnp.mean(x2 * x2, axis=-1, keepdims=True)
            x2 = x2 * lax.rsqrt(ms + EPS) * gfin_ref[...]
        o_ref[...] = x2


def _mix_mlp(x2d, attn, ssm, proj, w_out, g_mlp, w1, w2, g_final, final_norm):
    t, d = x2d.shape
    n_ff_steps = D_FF // MLP_TF
    ga_blk = (QK_WIDTH * 2 + ATTN_WIDTH) // D_MODEL
    row = lambda i, j: (i, 0)
    const = lambda i, j: (0, 0)
    return pl.pallas_call(
        functools.partial(_mix_mlp_kernel, n_ff_steps=n_ff_steps, final_norm=final_norm),
        grid=(t // MLP_TM, n_ff_steps),
        in_specs=[
            pl.BlockSpec((MLP_TM, d), row),
            pl.BlockSpec((MLP_TM, d), row),
            pl.BlockSpec((MLP_TM, d), row),
            pl.BlockSpec((MLP_TM, d), lambda i, j: (i, ga_blk)),
            pl.BlockSpec((MLP_TM, d), lambda i, j: (i, ga_blk + 1)),
            pl.BlockSpec((d, d), const),
            pl.BlockSpec((1, d), const),
            pl.BlockSpec((d, MLP_TF), lambda i, j: (0, j)),
            pl.BlockSpec((MLP_TF, d), lambda i, j: (j, 0)),
            pl.BlockSpec((1, d), const),
        ],
        out_specs=pl.BlockSpec((MLP_TM, d), row),
        out_shape=jax.ShapeDtypeStruct((t, d), F32),
        scratch_shapes=[
            pltpu.VMEM((MLP_TM, d), F32),
            pltpu.VMEM((MLP_TM, d), BF16),
            pltpu.VMEM((MLP_TM, d), F32),
        ],
        compiler_params=pltpu.CompilerParams(
            dimension_semantics=("arbitrary", "arbitrary"), vmem_limit_bytes=VMEM_LIMIT_BYTES),
        name="mix_mlp",
    )(x2d, attn, ssm, proj, proj, w_out, g_mlp.reshape(1, d), w1, w2, g_final.reshape(1, d))


def kernel(x, g_mix, w_in, lambda_q1, lambda_k1, lambda_q2, lambda_k2, subln_g, rel_bias, ssm_a_re, ssm_a_im,
           ssm_log_dt, ssm_b_re, ssm_b_im, ssm_c_re, ssm_c_im, ssm_d, w_glu, w_out, g_mlp, w1, w2, g_final):
    batch, seq, d = x.shape
    depth = w_in.shape[0]
    assert d == D_MODEL and seq % PROJ_TM == 0 and seq % ATT_T == 0 and seq % SSM_TL == 0
    assert batch == SUBLANES, "the SSM scan maps the batch onto the sublanes of one vreg"

    table = _t5_bucket_table(seq)
    far_bucket = int(table[ATT_T + 1])
    assert np.all(table[ATT_T + 1:] == far_bucket)
    rel0 = np.arange(ATT_T)[:, None] - np.arange(ATT_T)[None, :]
    idx0 = np.where(rel0 >= 0, table[np.maximum(rel0, 0)], -1)
    idx1 = table[rel0 + ATT_T]
    bucket_idx = jnp.asarray(np.stack([idx0, idx1]).astype(np.int32))
    bias_tiles = _bias_tiles(rel_bias.astype(F32), bucket_idx)

    x2d = x.reshape(batch * seq, d)
    for layer in range(depth):
        lam_init = 0.8 - 0.6 * math.exp(-0.3 * layer)
        lam = (jnp.exp(jnp.sum(lambda_q1[layer].astype(F32) * lambda_k1[layer].astype(F32)))
               - jnp.exp(jnp.sum(lambda_q2[layer].astype(F32) * lambda_k2[layer].astype(F32)))
               + lam_init).reshape(1)

        w = w_in[layer]
        u0 = 2 * QK_WIDTH + ATTN_WIDTH
        w_cat = jnp.concatenate([w[:, :u0], w[:, u0 + SSM_WIDTH:], w[:, u0:u0 + SSM_WIDTH]], axis=1).astype(BF16)
        proj, u_tb = _inproj(x2d, g_mix[layer], w_cat, batch, seq)

        attn = _attention(proj, lam, rel_bias.astype(F32), bias_tiles, subln_g[layer].astype(F32),
                          batch, seq, far_bucket, 1.0 - lam_init)

        wb, lam_b, wcr, wci = _ssm_params(ssm_a_re[layer], ssm_a_im[layer], ssm_log_dt[layer],
                                          ssm_b_re[layer], ssm_b_im[layer], ssm_c_re[layer], ssm_c_im[layer], batch)
        ssm = _ssm(u_tb.reshape(seq * batch, SSM_WIDTH), wb, lam_b, wcr, wci, ssm_d[layer].astype(F32),
                   w_glu[layer].astype(BF16), batch, seq)

        x2d = _mix_mlp(x2d, attn, ssm.reshape(batch * seq, d), proj, w_out[layer].astype(BF16),
                       g_mlp[layer].astype(F32), w1[layer].astype(BF16), w2[layer].astype(BF16),
                       g_final.astype(F32), final_norm=(layer == depth - 1))
    return x2d.reshape(batch, seq, d)
```
